```python
import math
import jax, jax.numpy as jnp
from jax import lax
import numpy as np

D_MODEL = 2048
BATCH = 2
SEQ = 4096
DEPTH = 1

N_META = 16
BLOCK_Q = 128
ROPE_THETA = 500000.0
EPS = 1e-6
N_DIFF_HEADS = 8
DIFF_HEAD_DIM = 64
DIFF_V_DIM = 2 * DIFF_HEAD_DIM
DIFF_ROT_DIM = DIFF_HEAD_DIM // 4
N_FOX_HEADS = 8
FOX_HEAD_DIM = 128
N_BRANCH = 2
DIFF_QK_W = N_DIFF_HEADS * 2 * DIFF_HEAD_DIM
DIFF_V_W = N_DIFF_HEADS * DIFF_V_DIM
FOX_W = N_FOX_HEADS * FOX_HEAD_DIM
W_IN_SPLITS = (DIFF_QK_W, DIFF_QK_W, DIFF_V_W, FOX_W, FOX_W, FOX_W, N_FOX_HEADS, D_MODEL, D_MODEL)
W_IN_COLS = sum(W_IN_SPLITS)
PEER_HEADS = 8
PEER_N_KEYS = 128
PEER_N_EXPERTS = PEER_N_KEYS * PEER_N_KEYS
PEER_KEY_DIM = 128
PEER_HALF = PEER_KEY_DIM // 2
PEER_TOPK = 16
PEER_CHUNK = 128

kernel_name = 'hybrid_diff_fox_peer_meta'


def rms_norm(x, g):
    x32 = x.astype(jnp.float32)
    y = x32 * lax.rsqrt(jnp.mean(x32 * x32, axis=-1, keepdims=True) + EPS)
    return (y * g.astype(jnp.float32)).astype(x.dtype)


def rope_tables(length, rot_dim):
    inv = ROPE_THETA ** (-jnp.arange(0, rot_dim, 2, dtype=jnp.float32) / rot_dim)
    ang = jnp.arange(length, dtype=jnp.float32)[:, None] * inv[None, :]
    return jnp.cos(ang), jnp.sin(ang)


def partial_rope(x, cos, sin):
    half = cos.shape[-1]
    r = 2 * half
    shape = (1, cos.shape[0]) + (1,) * (x.ndim - 3) + (half,)
    c, s = cos.reshape(shape), sin.reshape(shape)
    xr = x[..., :r].astype(jnp.float32)
    x1, x2 = xr[..., :half], xr[..., half:]
    rot = jnp.concatenate([x1 * c - x2 * s, x2 * c + x1 * s], axis=-1).astype(x.dtype)
    return jnp.concatenate([rot, x[..., r:]], axis=-1)


def block_ranges(length):
    n_blocks = (length - N_META) // BLOCK_Q
    return [(0, N_META)] + [(N_META + i * BLOCK_Q, N_META + (i + 1) * BLOCK_Q) for i in range(n_blocks)]


def causal_mask(lo, hi):
    qpos = jnp.arange(lo, hi)
    kpos = jnp.arange(hi)
    return kpos[None, :] <= qpos[:, None]


def diff_attention(q1, q2, k1, k2, v, lam):
    scale = DIFF_HEAD_DIM ** -0.5
    outs = []
    for lo, hi in block_ranges(q1.shape[1]):
        mask = causal_mask(lo, hi)

        def probs(q, k):
            s = jnp.einsum('bqhd,bkhd->bhqk', q[:, lo:hi], k[:, :hi]).astype(jnp.float32) * scale
            return jax.nn.softmax(jnp.where(mask, s, -jnp.inf), axis=-1)

        p = probs(q1, k1) - lam * probs(q2, k2)
        outs.append(jnp.einsum('bhqk,bkhd->bqhd', p.astype(v.dtype), v[:, :hi]))
    return jnp.concatenate(outs, axis=1)


def forgetting_attention(q, k, v, cum_logf):
    scale = FOX_HEAD_DIM ** -0.5
    outs = []
    for lo, hi in block_ranges(q.shape[1]):
        mask = causal_mask(lo, hi)
        s = jnp.einsum('bqhd,bkhd->bhqk', q[:, lo:hi], k[:, :hi]).astype(jnp.float32) * scale
        s = s + cum_logf[:, :, lo:hi, None] - cum_logf[:, :, None, :hi]
        p = jax.nn.softmax(jnp.where(mask, s, -jnp.inf), axis=-1)
        outs.append(jnp.einsum('bhqk,bkhd->bqhd', p.astype(v.dtype), v[:, :hi]))
    return jnp.concatenate(outs, axis=1)


def split_cols(y, sizes):
    points, acc = [], 0
    for sz in sizes[:-1]:
        acc += sz
        points.append(acc)
    return jnp.split(y, points, axis=-1)


def peer(h, w_query, sub_keys, expert_u, expert_v):
    b, length, d = h.shape
    t = h.reshape(b * length, d)
    n_tok = t.shape[0]
    q = (t @ w_query).reshape(n_tok, PEER_HEADS, 2, PEER_HALF)
    s = jnp.einsum('thcd,hcnd->thcn', q, sub_keys).astype(jnp.float32)
    v1, i1 = lax.top_k(s[:, :, 0], PEER_TOPK)
    v2, i2 = lax.top_k(s[:, :, 1], PEER_TOPK)
    cand = (v1[..., :, None] + v2[..., None, :]).reshape(n_tok, PEER_HEADS, PEER_TOPK * PEER_TOPK)
    best, j = lax.top_k(cand, PEER_TOPK)
    e_idx = (jnp.take_along_axis(i1, j // PEER_TOPK, axis=-1) * PEER_N_KEYS
             + jnp.take_along_axis(i2, j % PEER_TOPK, axis=-1))
    gate = jax.nn.softmax(best, axis=-1)
    pad = (-n_tok) % PEER_CHUNK
    tp = jnp.pad(t, ((0, pad), (0, 0))).reshape(-1, PEER_CHUNK, d)
    ep = jnp.pad(e_idx, ((0, pad), (0, 0), (0, 0))).reshape(-1, PEER_CHUNK, PEER_HEADS, PEER_TOPK)
    gp = jnp.pad(gate, ((0, pad), (0, 0), (0, 0))).reshape(-1, PEER_CHUNK, PEER_HEADS, PEER_TOPK)

    def chunk(args):
        tc, ec, gc = args
        u_sel = expert_u[ec]
        a = jax.nn.gelu(jnp.einsum('cd,chkd->chk', tc, u_sel).astype(jnp.float32), approximate=False)
        w = (gc * a).astype(tc.dtype)
        return jnp.einsum('chk,chkd->cd', w, expert_v[ec])

    out = lax.map(chunk, (tp, ep, gp))
    return out.reshape(-1, d)[:n_tok].reshape(b, length, d)


def setup_inputs(seed: int = 0) -> dict:
    key = jax.random.key(seed)
    ks = jax.random.split(key, 24)
    f32 = jnp.float32
    nrm = lambda k, shape, sc: jax.random.normal(k, shape, f32) * sc
    L_ = DEPTH
    return {
        'x': nrm(ks[0], (BATCH, SEQ, D_MODEL), 1.0),
        'meta_tokens': nrm(ks[1], (N_META, D_MODEL), 1.0),
        'norm_mix_g': 1.0 + nrm(ks[2], (L_, D_MODEL), 0.02),
        'w_in': nrm(ks[3], (L_, D_MODEL, W_IN_COLS), D_MODEL ** -0.5),
        'b_forget': 2.0 + nrm(ks[4], (L_, N_FOX_HEADS), 0.5),
        'b_gate': nrm(ks[5], (L_, N_BRANCH, D_MODEL), 0.02),
        'diff_q_norm_g': 1.0 + nrm(ks[6], (L_, DIFF_HEAD_DIM), 0.02),
        'diff_k_norm_g': 1.0 + nrm(ks[7], (L_, DIFF_HEAD_DIM), 0.02),
        'fox_q_norm_g': 1.0 + nrm(ks[8], (L_, FOX_HEAD_DIM), 0.02),
        'fox_k_norm_g': 1.0 + nrm(ks[9], (L_, FOX_HEAD_DIM), 0.02),
        'lambda_q1': nrm(ks[10], (L_, DIFF_HEAD_DIM), 0.1),
        'lambda_k1': nrm(ks[11], (L_, DIFF_HEAD_DIM), 0.1),
        'lambda_q2': nrm(ks[12], (L_, DIFF_HEAD_DIM), 0.1),
        'lambda_k2': nrm(ks[13], (L_, DIFF_HEAD_DIM), 0.1),
        'diff_subln_g': 1.0 + nrm(ks[14], (L_, DIFF_V_DIM), 0.02),
        'w_branch_diff': nrm(ks[15], (L_, DIFF_V_W, D_MODEL), DIFF_V_W ** -0.5),
        'w_branch_fox': nrm(ks[16], (L_, FOX_W, D_MODEL), FOX_W ** -0.5),
        'w_out': nrm(ks[17], (L_, D_MODEL, D_MODEL), D_MODEL ** -0.5),
        'norm_peer_g': 1.0 + nrm(ks[18], (L_, D_MODEL), 0.02),
        'peer_w_query': nrm(ks[19], (L_, D_MODEL, PEER_HEADS * PEER_KEY_DIM), D_MODEL ** -0.5),
        'peer_sub_keys': nrm(ks[20], (L_, PEER_HEADS, 2, PEER_N_KEYS, PEER_HALF), PEER_HALF ** -0.5),
        'peer_expert_u': nrm(ks[21], (L_, PEER_N_EXPERTS, D_MODEL), D_MODEL ** -0.5),
        'peer_expert_v': nrm(ks[22], (L_, PEER_N_EXPERTS, D_MODEL), PEER_HEADS ** -0.5),
    }


def reference(x, meta_tokens, norm_mix_g, w_in, b_forget, b_gate, diff_q_norm_g, diff_k_norm_g,
              fox_q_norm_g, fox_k_norm_g, lambda_q1, lambda_k1, lambda_q2, lambda_k2, diff_subln_g,
              w_branch_diff, w_branch_fox, w_out, norm_peer_g, peer_w_query, peer_sub_keys,
              peer_expert_u, peer_expert_v):
    b = x.shape[0]
    meta = jnp.broadcast_to(meta_tokens.astype(x.dtype)[None], (b, N_META, D_MODEL))
    h_res = jnp.concatenate([meta, x], axis=1)
    length = h_res.shape[1]
    cos, sin = rope_tables(length, DIFF_ROT_DIM)
    for l in range(DEPTH):
        h = rms_norm(h_res, norm_mix_g[l])
        dq, dk, dv, fq, fk, fv, f_logit, g_a, g_b = split_cols(h @ w_in[l], W_IN_SPLITS)
        dq = partial_rope(rms_norm(dq.reshape(b, length, N_DIFF_HEADS, 2, DIFF_HEAD_DIM), diff_q_norm_g[l]), cos, sin)
        dk = partial_rope(rms_norm(dk.reshape(b, length, N_DIFF_HEADS, 2, DIFF_HEAD_DIM), diff_k_norm_g[l]), cos, sin)
        dv = dv.reshape(b, length, N_DIFF_HEADS, DIFF_V_DIM)
        lam_init = 0.8 - 0.6 * math.exp(-0.3 * l)
        lam = (jnp.exp(jnp.sum(lambda_q1[l].astype(jnp.float32) * lambda_k1[l].astype(jnp.float32)))
               - jnp.exp(jnp.sum(lambda_q2[l].astype(jnp.float32) * lambda_k2[l].astype(jnp.float32)))
               + lam_init)
        o_diff = diff_attention(dq[..., 0, :], dq[..., 1, :], dk[..., 0, :], dk[..., 1, :], dv, lam)
        o_diff = rms_norm(o_diff, diff_subln_g[l]) * (1.0 - lam_init)
        y_diff = o_diff.reshape(b, length, DIFF_V_W) @ w_branch_diff[l]
        fq = rms_norm(fq.reshape(b, length, N_FOX_HEADS, FOX_HEAD_DIM), fox_q_norm_g[l])
        fk = rms_norm(fk.reshape(b, length, N_FOX_HEADS, FOX_HEAD_DIM), fox_k_norm_g[l])
        fv = fv.reshape(b, length, N_FOX_HEADS, FOX_HEAD_DIM)
        log_f = jax.nn.log_sigmoid((f_logit + b_forget[l]).astype(jnp.float32))
        cum_logf = jnp.transpose(jnp.cumsum(log_f, axis=1), (0, 2, 1))
        o_fox = forgetting_attention(fq, fk, fv, cum_logf)
        y_fox = o_fox.reshape(b, length, FOX_W) @ w_branch_fox[l]
        merged = jax.nn.sigmoid(g_a + b_gate[l, 0]) * y_diff + jax.nn.sigmoid(g_b + b_gate[l, 1]) * y_fox
        h_res = h_res + merged @ w_out[l]
        h2 = rms_norm(h_res, norm_peer_g[l])
        h_res = h_res + peer(h2, peer_w_query[l], peer_sub_keys[l], peer_expert_u[l], peer_expert_v[l])
    return h_res[:, N_META:]
```

```python
import functools
import math

import jax
import jax.numpy as jnp
from jax import lax
from jax.experimental import pallas as pl
from jax.experimental.pallas import tpu as pltpu

N_META = 16
ROPE_THETA = 500000.0
EPS = 1e-6
N_HEADS = 8
HEAD_W = 128
DIFF_HEAD_DIM = 64
DIFF_ROT_HALF = 8
GROUP_W = N_HEADS * HEAD_W
PEER_HEADS = 8
PEER_N_KEYS = 128
PEER_HALF = 64
PEER_TOPK = 16
LAM_INIT = 0.8 - 0.6 * math.exp(-0.3 * 0)

LANES = 128
VMEM_LIMIT = 58 * 1024 * 1024

F32 = jnp.float32
BF16 = jnp.bfloat16
NEG_INF = float("-inf")


def _cparams(sem):
    return pltpu.CompilerParams(dimension_semantics=sem, vmem_limit_bytes=VMEM_LIMIT)


def _split_bf16(x):
    hi = x.astype(BF16)
    lo = (x - hi.astype(F32)).astype(BF16)
    return hi, lo


def _dot(a, b):
    return jnp.dot(a, b, preferred_element_type=F32)


def _dot_nt(a, b):
    return lax.dot_general(a, b, (((1,), (1,)), ((), ())), preferred_element_type=F32)


def _group_mean_sq(x, ones_bd, width):
    hi, lo = _split_bf16(x * x)
    return (_dot(hi, ones_bd) + _dot(lo, ones_bd)) * (1.0 / width)


def _proj_kernel(x_ref, g_ref, w_ref, wfh_ref, wfl_ref, gq_ref, gk_ref, gfq_ref, gfk_ref,
                 rc_ref, ra_ref, rb_ref, bd64_ref, bd128_ref,
                 h_ref, p_ref, f_ref, hs_ref):
    n = pl.program_id(1)

    @pl.when(n == 0)
    def _():
        x = x_ref[...]
        y = x * lax.rsqrt(jnp.mean(x * x, axis=-1, keepdims=True) + EPS) * g_ref[...]
        hi, lo = _split_bf16(y)
        hs_ref[...] = hi
        h_ref[...] = hi
        f_ref[...] = _dot(hi, wfh_ref[...]) + _dot(lo, wfh_ref[...]) + _dot(hi, wfl_ref[...])

    y = _dot(hs_ref[...], w_ref[...])

    def diff_qk(g_row, scale):
        for h in range(N_HEADS):
            v = y[:, h * HEAD_W:(h + 1) * HEAD_W]
            v = v * lax.rsqrt(_group_mean_sq(v, bd64_ref[...], DIFF_HEAD_DIM) + EPS) * g_row
            v = (v * rc_ref[...] + pltpu.roll(v, HEAD_W - DIFF_ROT_HALF, 1) * ra_ref[...]
                 + pltpu.roll(v, DIFF_ROT_HALF, 1) * rb_ref[...])
            p_ref[:, h * HEAD_W:(h + 1) * HEAD_W] = (v * scale).astype(BF16)

    def fox_qk(g_row, scale):
        for h in range(N_HEADS):
            v = y[:, h * HEAD_W:(h + 1) * HEAD_W]
            v = v * lax.rsqrt(_group_mean_sq(v, bd128_ref[...], HEAD_W) + EPS) * g_row
            p_ref[:, h * HEAD_W:(h + 1) * HEAD_W] = (v * scale).astype(BF16)

    @pl.when(n == 0)
    def _():
        diff_qk(gq_ref[...], DIFF_HEAD_DIM ** -0.5)

    @pl.when(n == 1)
    def _():
        diff_qk(gk_ref[...], 1.0)

    @pl.when(n == 3)
    def _():
        fox_qk(gfq_ref[...], HEAD_W ** -0.5)

    @pl.when(n == 4)
    def _():
        fox_qk(gfk_ref[...], 1.0)

    @pl.when((n == 2) | (n == 5))
    def _():
        p_ref[...] = y.astype(BF16)


def _proj_call(x2d, g_row, w_qkv, wf_hi, wf_lo, gq, gk, gfq, gfk, rc, ra, rb, bd64, bd128, tm):
    m, d = x2d.shape
    n_groups = w_qkv.shape[1] // GROUP_W
    n_pos_blocks = rc.shape[0] // tm
    row = lambda i, n: (i, 0)
    const = lambda i, n: (0, 0)
    pos = lambda i, n: (i % n_pos_blocks, 0)
    return pl.pallas_call(
        _proj_kernel,
        grid=(m // tm, n_groups),
        in_specs=[
            pl.BlockSpec((tm, d), row),
            pl.BlockSpec((1, d), const),
            pl.BlockSpec((d, GROUP_W), lambda i, n: (0, n)),
            pl.BlockSpec((d, LANES), const),
            pl.BlockSpec((d, LANES), const),
            pl.BlockSpec((1, HEAD_W), const),
            pl.BlockSpec((1, HEAD_W), const),
            pl.BlockSpec((1, HEAD_W), const),
            pl.BlockSpec((1, HEAD_W), const),
            pl.BlockSpec((tm, HEAD_W), pos),
            pl.BlockSpec((tm, HEAD_W), pos),
            pl.BlockSpec((tm, HEAD_W), pos),
            pl.BlockSpec((HEAD_W, HEAD_W), const),
            pl.BlockSpec((HEAD_W, HEAD_W), const),
        ],
        out_specs=[
            pl.BlockSpec((tm, d), row),
            pl.BlockSpec((tm, GROUP_W), lambda i, n: (i, n)),
            pl.BlockSpec((tm, LANES), row),
        ],
        out_shape=[
            jax.ShapeDtypeStruct((m, d), BF16),
            jax.ShapeDtypeStruct((m, n_groups * GROUP_W), BF16),
            jax.ShapeDtypeStruct((m, LANES), F32),
        ],
        scratch_shapes=[pltpu.VMEM((tm, d), BF16)],
        compiler_params=_cparams(("parallel", "arbitrary")),
        name="proj",
    )(x2d, g_row, w_qkv, wf_hi, wf_lo, gq, gk, gfq, gfk, rc, ra, rb, bd64, bd128)


def _cumsum_kernel(f_ref, b_ref, tri_ref, c_ref):
    n_blocks = f_ref.shape[2] // LANES
    carry = jnp.zeros((N_HEADS, 1), F32)
    for j in range(n_blocks):
        z = f_ref[0, :, j * LANES:(j + 1) * LANES] + b_ref[...]
        lf = jnp.minimum(z, 0.0) - jnp.log1p(jnp.exp(-jnp.abs(z)))
        hi, lo = _split_bf16(lf)
        lo2 = (lf - hi.astype(F32) - lo.astype(F32)).astype(BF16)
        tri = tri_ref[...]
        c = _dot(hi, tri) + _dot(lo, tri) + _dot(lo2, tri) + carry
        c_ref[0, :, j * LANES:(j + 1) * LANES] = c
        carry = c[:, LANES - 1:LANES]


def _cumsum_call(f_t, b_col, tri):
    b, h, lk = f_t.shape
    return pl.pallas_call(
        _cumsum_kernel,
        grid=(b,),
        in_specs=[
            pl.BlockSpec((1, h, lk), lambda i: (i, 0, 0)),
            pl.BlockSpec((h, 1), lambda i: (0, 0)),
            pl.BlockSpec((LANES, LANES), lambda i: (0, 0)),
        ],
        out_specs=pl.BlockSpec((1, h, lk), lambda i: (i, 0, 0)),
        out_shape=jax.ShapeDtypeStruct((b, h, lk), F32),
        compiler_params=_cparams(("parallel",)),
        name="cumsum",
    )(f_t, b_col, tri)


def _online_step(s, v, state):
    m, l, acc = state
    m_new = jnp.maximum(m, jnp.max(s, axis=1, keepdims=True))
    alpha = jnp.exp(m - m_new)
    p = jnp.exp(s - m_new)
    l_new = alpha * l + jnp.sum(p, axis=1, keepdims=True)
    acc_new = alpha * acc + _dot(p.astype(BF16), v)
    return m_new, l_new, acc_new


def _init_state(tq):
    return (jnp.full((tq, 1), NEG_INF, F32), jnp.zeros((tq, 1), F32), jnp.zeros((tq, HEAD_W), F32))


def _fox_kernel(q_ref, k_ref, v_ref, km_ref, vm_ref, cq_ref, ck_ref, cm_ref, o_ref, *, tq):
    h = pl.program_id(1)
    i = pl.program_id(2)
    q = q_ref[...]
    lane8 = lax.broadcasted_iota(jnp.int32, cq_ref.shape[1:], 1)
    cq = jnp.sum(jnp.where(lane8 == h, cq_ref[0], 0.0), axis=1, keepdims=True)

    s = _dot_nt(q, km_ref[...]) + cq - cm_ref[0, 0]
    lane = lax.broadcasted_iota(jnp.int32, s.shape, 1)
    state = _online_step(jnp.where(lane < N_META, s, NEG_INF), vm_ref[...], _init_state(tq))

    def body(j, st):
        start = pl.multiple_of(j * tq, tq)
        sj = _dot_nt(q, k_ref[pl.ds(start, tq), :]) + cq - ck_ref[0, 0, :, pl.ds(start, tq)]
        return _online_step(sj, v_ref[pl.ds(start, tq), :], st)

    state = lax.fori_loop(0, i, body, state)

    start = pl.multiple_of(i * tq, tq)
    sd = _dot_nt(q, k_ref[pl.ds(start, tq), :]) + cq - ck_ref[0, 0, :, pl.ds(start, tq)]
    rows = lax.broadcasted_iota(jnp.int32, sd.shape, 0)
    cols = lax.broadcasted_iota(jnp.int32, sd.shape, 1)
    _, l, acc = _online_step(jnp.where(cols <= rows, sd, NEG_INF), v_ref[pl.ds(start, tq), :], state)
    o_ref[...] = (acc / l).astype(BF16)


def _diff_kernel(q_ref, k_ref, v_ref, km_ref, vm_ref, lq1_ref, lk1_ref, lq2_ref, lk2_ref, g_ref,
                 o_ref, *, tq):
    i = pl.program_id(2)
    q = q_ref[...]
    qlane = lax.broadcasted_iota(jnp.int32, q.shape, 1)
    zero = jnp.zeros_like(q)
    q1 = jnp.where(qlane < DIFF_HEAD_DIM, q, zero)
    q2 = jnp.where(qlane >= DIFF_HEAD_DIM, q, zero)

    km = km_ref[...]
    s1 = _dot_nt(q1, km)
    s2 = _dot_nt(q2, km)
    lane = lax.broadcasted_iota(jnp.int32, s1.shape, 1)
    vm = vm_ref[...]
    st1 = _online_step(jnp.where(lane < N_META, s1, NEG_INF), vm, _init_state(tq))
    st2 = _online_step(jnp.where(lane < N_META, s2, NEG_INF), vm, _init_state(tq))

    def body(j, sts):
        a, b = sts
        start = pl.multiple_of(j * tq, tq)
        k = k_ref[pl.ds(start, tq), :]
        v = v_ref[pl.ds(start, tq), :]
        return _online_step(_dot_nt(q1, k), v, a), _online_step(_dot_nt(q2, k), v, b)

    st1, st2 = lax.fori_loop(0, i, body, (st1, st2))

    start = pl.multiple_of(i * tq, tq)
    k = k_ref[pl.ds(start, tq), :]
    v = v_ref[pl.ds(start, tq), :]
    rows = lax.broadcasted_iota(jnp.int32, (tq, tq), 0)
    cols = lax.broadcasted_iota(jnp.int32, (tq, tq), 1)
    causal = cols <= rows
    _, l1, acc1 = _online_step(jnp.where(causal, _dot_nt(q1, k), NEG_INF), v, st1)
    _, l2, acc2 = _online_step(jnp.where(causal, _dot_nt(q2, k), NEG_INF), v, st2)

    lam = (jnp.exp(jnp.sum(lq1_ref[...] * lk1_ref[...], axis=1, keepdims=True))
           - jnp.exp(jnp.sum(lq2_ref[...] * lk2_ref[...], axis=1, keepdims=True)) + LAM_INIT)
    o = acc1 / l1 - lam * (acc2 / l2)
    o = o * lax.rsqrt(jnp.mean(o * o, axis=-1, keepdims=True) + EPS) * g_ref[...]
    o_ref[...] = (o * (1.0 - LAM_INIT)).astype(BF16)


def _attn_call(kernel, p, p_meta, q_group, k_group, v_group, extra, extra_specs, seq, tq, name):
    m = p.shape[0]
    b = m // seq
    nq = seq // tq
    hb = GROUP_W // HEAD_W
    in_specs = [
        pl.BlockSpec((tq, HEAD_W), lambda bi, h, i: (bi * nq + i, q_group * hb + h)),
        pl.BlockSpec((seq, HEAD_W), lambda bi, h, i: (bi, k_group * hb + h)),
        pl.BlockSpec((seq, HEAD_W), lambda bi, h, i: (bi, v_group * hb + h)),
        pl.BlockSpec((LANES, HEAD_W), lambda bi, h, i: (0, k_group * hb + h)),
        pl.BlockSpec((LANES, HEAD_W), lambda bi, h, i: (0, v_group * hb + h)),
    ] + extra_specs
    return pl.pallas_call(
        functools.partial(kernel, tq=tq),
        grid=(b, N_HEADS, nq),
        in_specs=in_specs,
        out_specs=pl.BlockSpec((tq, HEAD_W), lambda bi, h, i: (bi * nq + i, h)),
        out_shape=jax.ShapeDtypeStruct((m, GROUP_W), BF16),
        compiler_params=_cparams(("parallel", "parallel", "arbitrary")),
        name=name,
    )(p, p, p, p_meta, p_meta, *extra)


def _merge_kernel(od_ref, of_ref, h_ref, wbd_ref, wbf_ref, wga_ref, wgb_ref, ba_ref, bb_ref, o_ref):
    h = h_ref[...]
    ga = jax.nn.sigmoid(_dot(h, wga_ref[...]) + ba_ref[...])
    gb = jax.nn.sigmoid(_dot(h, wgb_ref[...]) + bb_ref[...])
    y = ga * _dot(od_ref[...], wbd_ref[...]) + gb * _dot(of_ref[...], wbf_ref[...])
    o_ref[...] = y.astype(BF16)


def _merge_call(od, of, h, wbd, wbf, wga, wgb, ba, bb, tm, tn):
    m, d = h.shape
    kb = od.shape[1]
    row = lambda i, n: (i, 0)
    col = lambda i, n: (0, n)
    return pl.pallas_call(
        _merge_kernel,
        grid=(m // tm, d // tn),
        in_specs=[
            pl.BlockSpec((tm, kb), row),
            pl.BlockSpec((tm, kb), row),
            pl.BlockSpec((tm, d), row),
            pl.BlockSpec((kb, tn), col),
            pl.BlockSpec((kb, tn), col),
            pl.BlockSpec((d, tn), col),
            pl.BlockSpec((d, tn), col),
            pl.BlockSpec((1, tn), col),
            pl.BlockSpec((1, tn), col),
        ],
        out_specs=pl.BlockSpec((tm, tn), lambda i, n: (i, n)),
        out_shape=jax.ShapeDtypeStruct((m, d), BF16),
        compiler_params=_cparams(("parallel", "arbitrary")),
        name="merge",
    )(od, of, h, wbd, wbf, wga, wgb, ba, bb)


def _out_kernel(mg_ref, w_ref, x_ref, g_ref, r_ref, h2_ref):
    r = x_ref[...] + _dot(mg_ref[...], w_ref[...])
    r_ref[...] = r
    h2 = r * lax.rsqrt(jnp.mean(r * r, axis=-1, keepdims=True) + EPS) * g_ref[...]
    h2_ref[...] = h2.astype(BF16)


def _out_call(mg, w_out, x2d, g_row, tm):
    m, d = x2d.shape
    row = lambda i: (i, 0)
    const = lambda i: (0, 0)
    return pl.pallas_call(
        _out_kernel,
        grid=(m // tm,),
        in_specs=[
            pl.BlockSpec((tm, d), row),
            pl.BlockSpec((d, d), const),
            pl.BlockSpec((tm, d), row),
            pl.BlockSpec((1, d), const),
        ],
        out_specs=[pl.BlockSpec((tm, d), row), pl.BlockSpec((tm, d), row)],
        out_shape=[jax.ShapeDtypeStruct((m, d), F32), jax.ShapeDtypeStruct((m, d), BF16)],
        compiler_params=_cparams(("parallel",)),
        name="outproj",
    )(mg, w_out, x2d, g_row)


def _top_values(s, k):
    rows = []
    for _ in range(k):
        mx = jnp.max(s, axis=0, keepdims=True)
        rows.append(mx)
        s = jnp.where(s == mx, NEG_INF, s)
    return rows


def _stack_rows(rows):
    idx = lax.broadcasted_iota(jnp.int32, (len(rows), rows[0].shape[1]), 0)
    out = jnp.broadcast_to(rows[0], idx.shape)
    for k in range(1, len(rows)):
        out = jnp.where(idx == k, rows[k], out)
    return out


def _select_kernel(h2_ref, wq_ref, kh_ref, kl_ref, s1_ref, e1_ref, s2_ref, e2_ref, thr_ref):
    q = _dot(h2_ref[...], wq_ref[...])
    thr_rows = []
    for h in range(PEER_HEADS):
        qh_hi, qh_lo = _split_bf16(q[:, h * HEAD_W:(h + 1) * HEAD_W])
        kh, kl = kh_ref[h], kl_ref[h]
        s = _dot_nt(kh, qh_hi) + _dot_nt(kh, qh_lo) + _dot_nt(kl, qh_hi)
        s1 = s[:PEER_N_KEYS]
        s2 = s[PEER_N_KEYS:]
        v1 = _top_values(s1, PEER_TOPK)
        v2 = _top_values(s2, PEER_TOPK)
        v1_all, v2_all = _stack_rows(v1), _stack_rows(v2)
        half = PEER_TOPK // 2
        cands = ([v1[0] + v2_all] + [v1[a] + v2_all[:half] for a in range(1, half)]
                 + [v1_all[half:] + v2[0]])
        best = _top_values(jnp.concatenate(cands, axis=0), PEER_TOPK)
        z = sum(jnp.exp(bk - best[0]) for bk in best)
        thr_rows.append(best[-1])
        s1_ref[h] = s1
        s2_ref[h] = s2
        e1_ref[h] = jnp.exp(s1 - v1[0]) / z
        e2_ref[h] = jnp.exp(s2 - v2[0])
    thr_ref[...] = _stack_rows(thr_rows)


def _select_call(h2, wq, kbd_hi, kbd_lo, tm):
    m, d = h2.shape
    big = jax.ShapeDtypeStruct((PEER_HEADS, PEER_N_KEYS, m), F32)
    big_spec = pl.BlockSpec((PEER_HEADS, PEER_N_KEYS, tm), lambda i: (0, 0, i))
    return pl.pallas_call(
        _select_kernel,
        grid=(m // tm,),
        in_specs=[
            pl.BlockSpec((tm, d), lambda i: (i, 0)),
            pl.BlockSpec(wq.shape, lambda i: (0, 0)),
            pl.BlockSpec(kbd_hi.shape, lambda i: (0, 0, 0)),
            pl.BlockSpec(kbd_lo.shape, lambda i: (0, 0, 0)),
        ],
        out_specs=[big_spec, big_spec, big_spec, big_spec,
                   pl.BlockSpec((PEER_HEADS, tm), lambda i: (0, i))],
        out_shape=[big, big, big, big, jax.ShapeDtypeStruct((PEER_HEADS, m), F32)],
        compiler_params=_cparams(("parallel",)),
        name="peer_select",
    )(h2, wq, kbd_hi, kbd_lo)


PEER_SUB = 32


def _peer_kernel(h2_ref, u_ref, vt_ref, s1_ref, e1_ref, s2_ref, e2_ref, thr_ref, r_ref,
                 o_ref, acc_ref, a_ref, w_ref, *, keys_per_chunk):
    j = pl.program_id(1)

    @pl.when(j == 0)
    def _():
        acc_ref[...] = jnp.zeros_like(acc_ref)

    a_ref[...] = _dot_nt(u_ref[...], h2_ref[...])

    subs = PEER_N_KEYS // PEER_SUB

    def body(it, carry):
        r = it // subs
        sub = it % subs
        k2 = pl.multiple_of(sub * PEER_SUB, PEER_SUB)
        g = jnp.zeros((PEER_SUB, a_ref.shape[1]), F32)
        for h in range(PEER_HEADS):
            s = s1_ref[h, pl.ds(r, 1), :] + s2_ref[h, pl.ds(k2, PEER_SUB), :]
            val = e1_ref[h, pl.ds(r, 1), :] * e2_ref[h, pl.ds(k2, PEER_SUB), :]
            g = g + jnp.where(s >= thr_ref[h:h + 1, :], val, 0.0)
        row0 = pl.multiple_of(it * PEER_SUB, PEER_SUB)
        a = a_ref[pl.ds(row0, PEER_SUB), :]
        gelu = 0.5 * a * (1.0 + lax.erf(a * (2.0 ** -0.5)))
        w_ref[pl.ds(row0, PEER_SUB), :] = (gelu * g).astype(BF16)
        return carry

    lax.fori_loop(0, keys_per_chunk * subs, body, 0)
    acc_ref[...] += _dot(vt_ref[...], w_ref[...])

    @pl.when(j == pl.num_programs(1) - 1)
    def _():
        o_ref[...] = r_ref[...] + acc_ref[...].T


def _peer_call(h2, u, vt, s1, e1, s2, e2, thr, resid, tm, ce):
    m, d = h2.shape
    n_exp = u.shape[0]
    kpc = ce // PEER_N_KEYS
    row = lambda i, j: (i, 0)
    tok = lambda i, j: (0, 0, i)
    return pl.pallas_call(
        functools.partial(_peer_kernel, keys_per_chunk=kpc),
        grid=(m // tm, n_exp // ce),
        in_specs=[
            pl.BlockSpec((tm, d), row),
            pl.BlockSpec((ce, d), lambda i, j: (j, 0)),
            pl.BlockSpec((d, ce), lambda i, j: (0, j)),
            pl.BlockSpec((PEER_HEADS, kpc, tm), lambda i, j: (0, j, i)),
            pl.BlockSpec((PEER_HEADS, kpc, tm), lambda i, j: (0, j, i)),
            pl.BlockSpec((PEER_HEADS, PEER_N_KEYS, tm), tok),
            pl.BlockSpec((PEER_HEADS, PEER_N_KEYS, tm), tok),
            pl.BlockSpec((PEER_HEADS, tm), lambda i, j: (0, i)),
            pl.BlockSpec((tm, d), row),
        ],
        out_specs=pl.BlockSpec((tm, d), row),
        out_shape=jax.ShapeDtypeStruct((m, d), F32),
        scratch_shapes=[pltpu.VMEM((d, tm), F32), pltpu.VMEM((ce, tm), F32),
                        pltpu.VMEM((ce, tm), BF16)],
        compiler_params=_cparams(("parallel", "arbitrary")),
        name="peer_experts",
    )(h2, u, vt, s1, e1, s2, e2, thr, resid)


def _rope_tables(positions):
    rot = 2 * DIFF_ROT_HALF
    inv = ROPE_THETA ** (-jnp.arange(0, rot, 2, dtype=F32) / rot)
    ang = positions.astype(F32)[:, None] * inv[None, :]
    cos, sin = jnp.cos(ang), jnp.sin(ang)
    n = positions.shape[0]
    pad = jnp.zeros((n, DIFF_HEAD_DIM - rot), F32)
    zeros8 = jnp.zeros((n, DIFF_ROT_HALF), F32)
    rc = jnp.concatenate([cos, cos, pad + 1.0], axis=1)
    ra = jnp.concatenate([-sin, zeros8, pad], axis=1)
    rb = jnp.concatenate([zeros8, sin, pad], axis=1)
    return tuple(jnp.concatenate([t, t], axis=1) for t in (rc, ra, rb))


def _block_diag_ones(width):
    idx = jnp.arange(HEAD_W) // width
    return (idx[:, None] == idx[None, :]).astype(BF16)


def kernel(x, meta_tokens, norm_mix_g, w_in, b_forget, b_gate, diff_q_norm_g, diff_k_norm_g,
           fox_q_norm_g, fox_k_norm_g, lambda_q1, lambda_k1, lambda_q2, lambda_k2, diff_subln_g,
           w_branch_diff, w_branch_fox, w_out, norm_peer_g, peer_w_query, peer_sub_keys,
           peer_expert_u, peer_expert_v):
    b, seq, d = x.shape
    m = b * seq
    x2d = x.reshape(m, d)
    meta_pad = jnp.pad(meta_tokens.astype(F32), ((0, LANES - N_META), (0, 0)))

    qkv_w = 6 * GROUP_W
    w0 = w_in[0]
    w_qkv = w0[:, :qkv_w].astype(BF16)
    wf = jnp.pad(w0[:, qkv_w:qkv_w + N_HEADS], ((0, 0), (0, LANES - N_HEADS)))
    wf_hi, wf_lo = _split_bf16(wf)
    w_ga = w0[:, qkv_w + N_HEADS:qkv_w + N_HEADS + d].astype(BF16)
    w_gb = w0[:, qkv_w + N_HEADS + d:].astype(BF16)

    two = lambda g: jnp.concatenate([g, g]).reshape(1, HEAD_W)
    gq, gk = two(diff_q_norm_g[0]), two(diff_k_norm_g[0])
    gfq, gfk = fox_q_norm_g[0].reshape(1, HEAD_W), fox_k_norm_g[0].reshape(1, HEAD_W)
    bd64, bd128 = _block_diag_ones(DIFF_HEAD_DIM), _block_diag_ones(HEAD_W)
    g_mix = norm_mix_g[0].reshape(1, d)

    tm = min(512, seq)
    proj = functools.partial(_proj_call, g_row=g_mix, w_qkv=w_qkv, wf_hi=wf_hi, wf_lo=wf_lo,
                             gq=gq, gk=gk, gfq=gfq, gfk=gfk, bd64=bd64, bd128=bd128)
    rc, ra, rb = _rope_tables(N_META + jnp.arange(seq))
    h, p, f = proj(x2d, rc=rc, ra=ra, rb=rb, tm=tm)
    rc, ra, rb = _rope_tables(jnp.arange(LANES))
    _, p_meta, f_meta = proj(meta_pad, rc=rc, ra=ra, rb=rb, tm=LANES)

    lk = seq + LANES
    f_real = f[:, :N_HEADS].reshape(b, seq, N_HEADS).transpose(0, 2, 1)
    f_m = jnp.broadcast_to(f_meta[:N_META, :N_HEADS].T[None], (b, N_HEADS, N_META))
    f_t = jnp.concatenate([f_m, f_real, jnp.zeros((b, N_HEADS, LANES - N_META), F32)], axis=2)
    tri = (jnp.arange(LANES)[:, None] <= jnp.arange(LANES)[None, :]).astype(BF16)
    c = _cumsum_call(f_t, b_forget[0].reshape(N_HEADS, 1), tri)
    c_real = c[:, :, N_META:N_META + seq]
    ck = c_real.reshape(b, N_HEADS, 1, seq)
    cm = jnp.pad(c[:, :, :N_META], ((0, 0), (0, 0), (0, LANES - N_META))).reshape(b, N_HEADS, 1, LANES)
    cq = c_real.transpose(0, 2, 1)

    tq = min(256, seq)
    o_fox = _attn_call(
        _fox_kernel, p, p_meta, 3, 4, 5, (cq, ck, cm),
        [pl.BlockSpec((1, tq, N_HEADS), lambda bi, hh, i: (bi, i, 0)),
         pl.BlockSpec((1, 1, 1, seq), lambda bi, hh, i: (bi, hh, 0, 0)),
         pl.BlockSpec((1, 1, 1, LANES), lambda bi, hh, i: (bi, hh, 0, 0))],
        seq, tq, "fox_attn")
    vec = lambda a: a[0].reshape(1, -1)
    const3 = lambda bi, hh, i: (0, 0)
    o_diff = _attn_call(
        _diff_kernel, p, p_meta, 0, 1, 2,
        (vec(lambda_q1), vec(lambda_k1), vec(lambda_q2), vec(lambda_k2), vec(diff_subln_g)),
        [pl.BlockSpec((1, DIFF_HEAD_DIM), const3)] * 4 + [pl.BlockSpec((1, HEAD_W), const3)],
        seq, tq, "diff_attn")

    merged = _merge_call(o_diff, o_fox, h, w_branch_diff[0].astype(BF16), w_branch_fox[0].astype(BF16),
                         w_ga, w_gb, b_gate[0, 0].reshape(1, d), b_gate[0, 1].reshape(1, d),
                         tm=tm, tn=512)
    resid, h2 = _out_call(merged, w_out[0].astype(BF16), x2d, norm_peer_g[0].reshape(1, d),
                          tm=min(256, seq))

    sk = peer_sub_keys[0]
    zk = jnp.zeros_like(sk[:, 0])
    kbd = jnp.concatenate([jnp.concatenate([sk[:, 0], zk], axis=2),
                           jnp.concatenate([zk, sk[:, 1]], axis=2)], axis=1)
    kbd_hi, kbd_lo = _split_bf16(kbd)
    s1, e1, s2, e2, thr = _select_call(h2, peer_w_query[0].astype(BF16), kbd_hi, kbd_lo,
                                       tm=min(256, seq))

    out = _peer_call(h2, peer_expert_u[0].astype(BF16), peer_expert_v[0].astype(BF16).T,
                     s1, e1, s2, e2, thr, resid, tm=tm, ce=1024)
    return out.reshape(b, seq, d)
```

```python
import functools
import math

import jax
import jax.numpy as jnp
from jax import lax
from jax.experimental import pallas as pl
from jax.experimental.pallas import tpu as pltpu

N_META = 16
ROPE_THETA = 500000.0
EPS = 1e-6
N_HEADS = 8
HEAD_W = 128
DIFF_HEAD_DIM = 64
DIFF_ROT_HALF = 8
GROUP_W = N_HEADS * HEAD_W
PEER_HEADS = 8
PEER_N_KEYS = 128
PEER_HALF = 64
PEER_TOPK = 16
LAM_INIT = 0.8 - 0.6 * math.exp(-0.3 * 0)

LANES = 128
SUBLANES = 8
VMEM_LIMIT = 58 * 1024 * 1024

F32 = jnp.float32
BF16 = jnp.bfloat16
NEG_INF = float("-inf")


def _cparams(sem):
    return pltpu.CompilerParams(dimension_semantics=sem, vmem_limit_bytes=VMEM_LIMIT)


def _split_bf16(x):
    hi = x.astype(BF16)
    lo = (x - hi.astype(F32)).astype(BF16)
    return hi, lo


def _dot(a, b):
    return jnp.dot(a, b, preferred_element_type=F32)


def _dot_nt(a, b):
    return lax.dot_general(a, b, (((1,), (1,)), ((), ())), preferred_element_type=F32)


def _group_mean_sq(x, ones_bd, width):
    hi, lo = _split_bf16(x * x)
    return (_dot(hi, ones_bd) + _dot(lo, ones_bd)) * (1.0 / width)


def _proj_kernel(x_ref, g_ref, w_ref, wfh_ref, wfl_ref, gq_ref, gk_ref, gfq_ref, gfk_ref,
                 rc_ref, ra_ref, rb_ref, bd64_ref, bd128_ref,
                 h_ref, p_ref, f_ref, hs_ref):
    n = pl.program_id(1)

    @pl.when(n == 0)
    def _():
        x = x_ref[...]
        y = x * lax.rsqrt(jnp.mean(x * x, axis=-1, keepdims=True) + EPS) * g_ref[...]
        hi, lo = _split_bf16(y)
        hs_ref[...] = hi
        h_ref[...] = hi
        f_ref[...] = _dot(hi, wfh_ref[...]) + _dot(lo, wfh_ref[...]) + _dot(hi, wfl_ref[...])

    y = _dot(hs_ref[...], w_ref[...])

    def diff_qk(g_row, scale):
        for h in range(N_HEADS):
            v = y[:, h * HEAD_W:(h + 1) * HEAD_W]
            v = v * lax.rsqrt(_group_mean_sq(v, bd64_ref[...], DIFF_HEAD_DIM) + EPS) * g_row
            v = (v * rc_ref[...] + pltpu.roll(v, HEAD_W - DIFF_ROT_HALF, 1) * ra_ref[...]
                 + pltpu.roll(v, DIFF_ROT_HALF, 1) * rb_ref[...])
            p_ref[:, h * HEAD_W:(h + 1) * HEAD_W] = (v * scale).astype(BF16)

    def fox_qk(g_row, scale):
        for h in range(N_HEADS):
            v = y[:, h * HEAD_W:(h + 1) * HEAD_W]
            v = v * lax.rsqrt(_group_mean_sq(v, bd128_ref[...], HEAD_W) + EPS) * g_row
            p_ref[:, h * HEAD_W:(h + 1) * HEAD_W] = (v * scale).astype(BF16)

    @pl.when(n == 0)
    def _():
        diff_qk(gq_ref[...], DIFF_HEAD_DIM ** -0.5)

    @pl.when(n == 1)
    def _():
        diff_qk(gk_ref[...], 1.0)

    @pl.when(n == 3)
    def _():
        fox_qk(gfq_ref[...], HEAD_W ** -0.5)

    @pl.when(n == 4)
    def _():
        fox_qk(gfk_ref[...], 1.0)

    @pl.when((n == 2) | (n == 5))
    def _():
        p_ref[...] = y.astype(BF16)


def _proj_call(x2d, g_row, w_qkv, wf_hi, wf_lo, gq, gk, gfq, gfk, rc, ra, rb, bd64, bd128, tm):
    m, d = x2d.shape
    n_groups = w_qkv.shape[1] // GROUP_W
    n_pos_blocks = rc.shape[0] // tm
    row = lambda i, n: (i, 0)
    const = lambda i, n: (0, 0)
    pos = lambda i, n: (i % n_pos_blocks, 0)
    return pl.pallas_call(
        _proj_kernel,
        grid=(m // tm, n_groups),
        in_specs=[
            pl.BlockSpec((tm, d), row),
            pl.BlockSpec((1, d), const),
            pl.BlockSpec((d, GROUP_W), lambda i, n: (0, n)),
            pl.BlockSpec((d, LANES), const),
            pl.BlockSpec((d, LANES), const),
            pl.BlockSpec((1, HEAD_W), const),
            pl.BlockSpec((1, HEAD_W), const),
            pl.BlockSpec((1, HEAD_W), const),
            pl.BlockSpec((1, HEAD_W), const),
            pl.BlockSpec((tm, HEAD_W), pos),
            pl.BlockSpec((tm, HEAD_W), pos),
            pl.BlockSpec((tm, HEAD_W), pos),
            pl.BlockSpec((HEAD_W, HEAD_W), const),
            pl.BlockSpec((HEAD_W, HEAD_W), const),
        ],
        out_specs=[
            pl.BlockSpec((tm, d), row),
            pl.BlockSpec((tm, GROUP_W), lambda i, n: (i, n)),
            pl.BlockSpec((tm, LANES), row),
        ],
        out_shape=[
            jax.ShapeDtypeStruct((m, d), BF16),
            jax.ShapeDtypeStruct((m, n_groups * GROUP_W), BF16),
            jax.ShapeDtypeStruct((m, LANES), F32),
        ],
        scratch_shapes=[pltpu.VMEM((tm, d), BF16)],
        compiler_params=_cparams(("parallel", "arbitrary")),
        name="proj",
    )(x2d, g_row, w_qkv, wf_hi, wf_lo, gq, gk, gfq, gfk, rc, ra, rb, bd64, bd128)


def _cumsum_kernel(f_ref, b_ref, tri_ref, c_ref):
    n_blocks = f_ref.shape[2] // LANES
    carry = jnp.zeros((N_HEADS, 1), F32)
    for j in range(n_blocks):
        z = f_ref[0, :, j * LANES:(j + 1) * LANES] + b_ref[...]
        lf = jnp.minimum(z, 0.0) - jnp.log1p(jnp.exp(-jnp.abs(z)))
        hi, lo = _split_bf16(lf)
        lo2 = (lf - hi.astype(F32) - lo.astype(F32)).astype(BF16)
        tri = tri_ref[...]
        c = _dot(hi, tri) + _dot(lo, tri) + _dot(lo2, tri) + carry
        c_ref[0, :, j * LANES:(j + 1) * LANES] = c
        carry = c[:, LANES - 1:LANES]


def _cumsum_call(f_t, b_col, tri):
    b, h, lk = f_t.shape
    return pl.pallas_call(
        _cumsum_kernel,
        grid=(b,),
        in_specs=[
            pl.BlockSpec((1, h, lk), lambda i: (i, 0, 0)),
            pl.BlockSpec((h, 1), lambda i: (0, 0)),
            pl.BlockSpec((LANES, LANES), lambda i: (0, 0)),
        ],
        out_specs=pl.BlockSpec((1, h, lk), lambda i: (i, 0, 0)),
        out_shape=jax.ShapeDtypeStruct((b, h, lk), F32),
        compiler_params=_cparams(("parallel",)),
        name="cumsum",
    )(f_t, b_col, tri)


def _tree(op, parts):
    while len(parts) > 1:
        parts = [op(parts[k], parts[k + 1]) if k + 1 < len(parts) else parts[k]
                 for k in range(0, len(parts), 2)]
    return parts[0]


def _fold_rows(x, op):
    return _tree(op, [x[r:r + SUBLANES] for r in range(0, x.shape[0], SUBLANES)])


def _attn_passes(score_fns, vt_ref, vtm_ref, acc_refs, i, tq):
    macc = tuple(jnp.maximum(_fold_rows(f[0](), jnp.maximum), _fold_rows(f[2](), jnp.maximum))
                 for f in score_fns)

    def max_body(j, acc):
        return tuple(jnp.maximum(a, _fold_rows(f[1](j), jnp.maximum)) for a, f in zip(acc, score_fns))

    macc = lax.fori_loop(0, i, max_body, macc)
    m = tuple(jnp.max(a, axis=0, keepdims=True) for a in macc)

    def accumulate(s, mk, vt, acc_ref, first):
        p = jnp.exp(s - mk)
        pv = _dot(vt, p.astype(BF16))
        if first:
            acc_ref[...] = pv
        else:
            acc_ref[...] += pv
        return _fold_rows(p, jnp.add)

    lacc = tuple(accumulate(f[0](), mk, vtm_ref[0], r, True)
                 for f, mk, r in zip(score_fns, m, acc_refs))

    def sum_body(j, acc):
        vt = vt_ref[0, 0, :, pl.ds(pl.multiple_of(j * tq, tq), tq)]
        return tuple(a + accumulate(f[1](j), mk, vt, r, False)
                     for a, f, mk, r in zip(acc, score_fns, m, acc_refs))

    lacc = lax.fori_loop(0, i, sum_body, lacc)
    vt = vt_ref[0, 0, :, pl.ds(pl.multiple_of(i * tq, tq), tq)]
    lacc = tuple(a + accumulate(f[2](), mk, vt, r, False)
                 for a, f, mk, r in zip(lacc, score_fns, m, acc_refs))
    return tuple(jnp.sum(a, axis=0, keepdims=True) for a in lacc)


def _masks(tq):
    krow = lax.broadcasted_iota(jnp.int32, (tq, tq), 0)
    qcol = lax.broadcasted_iota(jnp.int32, (tq, tq), 1)
    meta_rows = lax.broadcasted_iota(jnp.int32, (LANES, tq), 0) < N_META
    return krow <= qcol, meta_rows


def _fox_kernel(q_ref, k_ref, vt_ref, km_ref, vtm_ref, cq_ref, ckb_ref, cmb_ref, o_ref, acc_ref,
                *, tq):
    i = pl.program_id(2)
    q = q_ref[...]
    cq = cq_ref[0, 0]
    causal, meta_rows = _masks(tq)

    def scores(k, cb):
        return _dot_nt(k, q) + cq - jnp.concatenate([cb] * (tq // LANES), axis=1)

    def block(j):
        start = pl.multiple_of(j * tq, tq)
        return scores(k_ref[pl.ds(start, tq), :], ckb_ref[0, 0, pl.ds(start, tq), :])

    fns = (lambda: jnp.where(meta_rows, scores(km_ref[...], cmb_ref[0, 0]), NEG_INF),
           block,
           lambda: jnp.where(causal, block(i), NEG_INF))
    (l,) = _attn_passes((fns,), vt_ref, vtm_ref, (acc_ref,), i, tq)
    o_ref[...] = (acc_ref[...] / l).T.astype(BF16)


def _diff_kernel(q_ref, k_ref, vt_ref, km_ref, vtm_ref, lq1_ref, lk1_ref, lq2_ref, lk2_ref, g_ref,
                 o_ref, acc1_ref, acc2_ref, *, tq):
    i = pl.program_id(2)
    q = q_ref[...]
    qlane = lax.broadcasted_iota(jnp.int32, q.shape, 1)
    zero = jnp.zeros_like(q)
    causal, meta_rows = _masks(tq)

    def make(qz):
        def block(j):
            return _dot_nt(k_ref[pl.ds(pl.multiple_of(j * tq, tq), tq), :], qz)
        return (lambda: jnp.where(meta_rows, _dot_nt(km_ref[...], qz), NEG_INF),
                block,
                lambda: jnp.where(causal, block(i), NEG_INF))

    fns = (make(jnp.where(qlane < DIFF_HEAD_DIM, q, zero)),
           make(jnp.where(qlane >= DIFF_HEAD_DIM, q, zero)))
    l1, l2 = _attn_passes(fns, vt_ref, vtm_ref, (acc1_ref, acc2_ref), i, tq)

    lam = (jnp.exp(jnp.sum(lq1_ref[...] * lk1_ref[...], axis=1, keepdims=True))
           - jnp.exp(jnp.sum(lq2_ref[...] * lk2_ref[...], axis=1, keepdims=True)) + LAM_INIT)
    o = (acc1_ref[...] / l1 - lam * (acc2_ref[...] / l2)).T
    o = o * lax.rsqrt(jnp.mean(o * o, axis=-1, keepdims=True) + EPS) * g_ref[...]
    o_ref[...] = (o * (1.0 - LAM_INIT)).astype(BF16)


def _attn_call(kernel, p, p_meta, vt, vtm, q_group, k_group, extra, extra_specs, n_acc, seq, tq,
               name):
    m = p.shape[0]
    b = m // seq
    nq = seq // tq
    hb = GROUP_W // HEAD_W
    in_specs = [
        pl.BlockSpec((tq, HEAD_W), lambda bi, h, i: (bi * nq + i, q_group * hb + h)),
        pl.BlockSpec((seq, HEAD_W), lambda bi, h, i: (bi, k_group * hb + h)),
        pl.BlockSpec((1, 1, HEAD_W, seq), lambda bi, h, i: (bi, h, 0, 0)),
        pl.BlockSpec((LANES, HEAD_W), lambda bi, h, i: (0, k_group * hb + h)),
        pl.BlockSpec((1, HEAD_W, LANES), lambda bi, h, i: (h, 0, 0)),
    ] + extra_specs
    return pl.pallas_call(
        functools.partial(kernel, tq=tq),
        grid=(b, N_HEADS, nq),
        in_specs=in_specs,
        out_specs=pl.BlockSpec((tq, HEAD_W), lambda bi, h, i: (bi * nq + i, h)),
        out_shape=jax.ShapeDtypeStruct((m, GROUP_W), BF16),
        scratch_shapes=[pltpu.VMEM((HEAD_W, tq), F32)] * n_acc,
        compiler_params=_cparams(("parallel", "parallel", "arbitrary")),
        name=name,
    )(p, p, vt, p_meta, vtm, *extra)


def _merge_kernel(od_ref, of_ref, h_ref, wbd_ref, wbf_ref, wga_ref, wgb_ref, ba_ref, bb_ref, o_ref):
    h = h_ref[...]
    ga = jax.nn.sigmoid(_dot(h, wga_ref[...]) + ba_ref[...])
    gb = jax.nn.sigmoid(_dot(h, wgb_ref[...]) + bb_ref[...])
    y = ga * _dot(od_ref[...], wbd_ref[...]) + gb * _dot(of_ref[...], wbf_ref[...])
    o_ref[...] = y.astype(BF16)


def _merge_call(od, of, h, wbd, wbf, wga, wgb, ba, bb, tm, tn):
    m, d = h.shape
    kb = od.shape[1]
    row = lambda i, n: (i, 0)
    col = lambda i, n: (0, n)
    return pl.pallas_call(
        _merge_kernel,
        grid=(m // tm, d // tn),
        in_specs=[
            pl.BlockSpec((tm, kb), row),
            pl.BlockSpec((tm, kb), row),
            pl.BlockSpec((tm, d), row),
            pl.BlockSpec((kb, tn), col),
            pl.BlockSpec((kb, tn), col),
            pl.BlockSpec((d, tn), col),
            pl.BlockSpec((d, tn), col),
            pl.BlockSpec((1, tn), col),
            pl.BlockSpec((1, tn), col),
        ],
        out_specs=pl.BlockSpec((tm, tn), lambda i, n: (i, n)),
        out_shape=jax.ShapeDtypeStruct((m, d), BF16),
        compiler_params=_cparams(("parallel", "arbitrary")),
        name="merge",
    )(od, of, h, wbd, wbf, wga, wgb, ba, bb)


def _out_kernel(mg_ref, w_ref, x_ref, g_ref, r_ref, h2_ref):
    r = x_ref[...] + _dot(mg_ref[...], w_ref[...])
    r_ref[...] = r
    h2 = r * lax.rsqrt(jnp.mean(r * r, axis=-1, keepdims=True) + EPS) * g_ref[...]
    h2_ref[...] = h2.astype(BF16)


def _out_call(mg, w_out, x2d, g_row, tm):
    m, d = x2d.shape
    row = lambda i: (i, 0)
    const = lambda i: (0, 0)
    return pl.pallas_call(
        _out_kernel,
        grid=(m // tm,),
        in_specs=[
            pl.BlockSpec((tm, d), row),
            pl.BlockSpec((d, d), const),
            pl.BlockSpec((tm, d), row),
            pl.BlockSpec((1, d), const),
        ],
        out_specs=[pl.BlockSpec((tm, d), row), pl.BlockSpec((tm, d), row)],
        out_shape=[jax.ShapeDtypeStruct((m, d), F32), jax.ShapeDtypeStruct((m, d), BF16)],
        compiler_params=_cparams(("parallel",)),
        name="outproj",
    )(mg, w_out, x2d, g_row)


def _top_values(s, k):
    rows = []
    for _ in range(k):
        mx = jnp.max(s, axis=0, keepdims=True)
        rows.append(mx)
        s = jnp.where(s == mx, NEG_INF, s)
    return rows


def _stack_rows(rows):
    idx = lax.broadcasted_iota(jnp.int32, (len(rows), rows[0].shape[1]), 0)
    out = jnp.broadcast_to(rows[0], idx.shape)
    for k in range(1, len(rows)):
        out = jnp.where(idx == k, rows[k], out)
    return out


def _select_kernel(h2_ref, wq_ref, kh_ref, kl_ref, s1_ref, e1_ref, s2_ref, e2_ref, thr_ref):
    q = _dot(h2_ref[...], wq_ref[...])
    thr_rows = []
    for h in range(PEER_HEADS):
        qh_hi, qh_lo = _split_bf16(q[:, h * HEAD_W:(h + 1) * HEAD_W])
        kh, kl = kh_ref[h], kl_ref[h]
        s = _dot_nt(kh, qh_hi) + _dot_nt(kh, qh_lo) + _dot_nt(kl, qh_hi)
        s1 = s[:PEER_N_KEYS]
        s2 = s[PEER_N_KEYS:]
        v1 = _top_values(s1, PEER_TOPK)
        v2 = _top_values(s2, PEER_TOPK)
        v1_all, v2_all = _stack_rows(v1), _stack_rows(v2)
        half = PEER_TOPK // 2
        cands = ([v1[0] + v2_all] + [v1[a] + v2_all[:half] for a in range(1, half)]
                 + [v1_all[half:] + v2[0]])
        best = _top_values(jnp.concatenate(cands, axis=0), PEER_TOPK)
        z = sum(jnp.exp(bk - best[0]) for bk in best)
        thr_rows.append(best[-1])
        s1_ref[h] = s1
        s2_ref[h] = s2
        e1_ref[h] = jnp.exp(s1 - v1[0]) / z
        e2_ref[h] = jnp.exp(s2 - v2[0])
    thr_ref[...] = _stack_rows(thr_rows)


def _select_call(h2, wq, kbd_hi, kbd_lo, tm):
    m, d = h2.shape
    big = jax.ShapeDtypeStruct((PEER_HEADS, PEER_N_KEYS, m), F32)
    big_spec = pl.BlockSpec((PEER_HEADS, PEER_N_KEYS, tm), lambda i: (0, 0, i))
    return pl.pallas_call(
        _select_kernel,
        grid=(m // tm,),
        in_specs=[
            pl.BlockSpec((tm, d), lambda i: (i, 0)),
            pl.BlockSpec(wq.shape, lambda i: (0, 0)),
            pl.BlockSpec(kbd_hi.shape, lambda i: (0, 0, 0)),
            pl.BlockSpec(kbd_lo.shape, lambda i: (0, 0, 0)),
        ],
        out_specs=[big_spec, big_spec, big_spec, big_spec,
                   pl.BlockSpec((PEER_HEADS, tm), lambda i: (0, i))],
        out_shape=[big, big, big, big, jax.ShapeDtypeStruct((PEER_HEADS, m), F32)],
        compiler_params=_cparams(("parallel",)),
        name="peer_select",
    )(h2, wq, kbd_hi, kbd_lo)


PEER_SUB = 32


def _peer_kernel(h2_ref, u_ref, vt_ref, s1_ref, e1_ref, s2_ref, e2_ref, thr_ref, r_ref,
                 o_ref, acc_ref, a_ref, w_ref, *, keys_per_chunk):
    j = pl.program_id(1)

    @pl.when(j == 0)
    def _():
        acc_ref[...] = jnp.zeros_like(acc_ref)

    a_ref[...] = _dot_nt(u_ref[...], h2_ref[...])

    subs = PEER_N_KEYS // PEER_SUB

    def body(it, carry):
        r = it // subs
        sub = it % subs
        k2 = pl.multiple_of(sub * PEER_SUB, PEER_SUB)
        g = jnp.zeros((PEER_SUB, a_ref.shape[1]), F32)
        for h in range(PEER_HEADS):
            s = s1_ref[h, pl.ds(r, 1), :] + s2_ref[h, pl.ds(k2, PEER_SUB), :]
            val = e1_ref[h, pl.ds(r, 1), :] * e2_ref[h, pl.ds(k2, PEER_SUB), :]
            g = g + jnp.where(s >= thr_ref[h:h + 1, :], val, 0.0)
        row0 = pl.multiple_of(it * PEER_SUB, PEER_SUB)
        a = a_ref[pl.ds(row0, PEER_SUB), :]
        gelu = 0.5 * a * (1.0 + lax.erf(a * (2.0 ** -0.5)))
        w_ref[pl.ds(row0, PEER_SUB), :] = (gelu * g).astype(BF16)
        return carry

    lax.fori_loop(0, keys_per_chunk * subs, body, 0)
    acc_ref[...] += _dot(vt_ref[...], w_ref[...])

    @pl.when(j == pl.num_programs(1) - 1)
    def _():
        o_ref[...] = r_ref[...] + acc_ref[...].T


def _peer_call(h2, u, vt, s1, e1, s2, e2, thr, resid, tm, ce):
    m, d = h2.shape
    n_exp = u.shape[0]
    kpc = ce // PEER_N_KEYS
    row = lambda i, j: (i, 0)
    tok = lambda i, j: (0, 0, i)
    return pl.pallas_call(
        functools.partial(_peer_kernel, keys_per_chunk=kpc),
        grid=(m // tm, n_exp // ce),
        in_specs=[
            pl.BlockSpec((tm, d), row),
            pl.BlockSpec((ce, d), lambda i, j: (j, 0)),
            pl.BlockSpec((d, ce), lambda i, j: (0, j)),
            pl.BlockSpec((PEER_HEADS, kpc, tm), lambda i, j: (0, j, i)),
            pl.BlockSpec((PEER_HEADS, kpc, tm), lambda i, j: (0, j, i)),
            pl.BlockSpec((PEER_HEADS, PEER_N_KEYS, tm), tok),
            pl.BlockSpec((PEER_HEADS, PEER_N_KEYS, tm), tok),
            pl.BlockSpec((PEER_HEADS, tm), lambda i, j: (0, i)),
            pl.BlockSpec((tm, d), row),
        ],
        out_specs=pl.BlockSpec((tm, d), row),
        out_shape=jax.ShapeDtypeStruct((m, d), F32),
        scratch_shapes=[pltpu.VMEM((d, tm), F32), pltpu.VMEM((ce, tm), F32),
                        pltpu.VMEM((ce, tm), BF16)],
        compiler_params=_cparams(("parallel", "arbitrary")),
        name="peer_experts",
    )(h2, u, vt, s1, e1, s2, e2, thr, resid)


def _rope_tables(positions):
    rot = 2 * DIFF_ROT_HALF
    inv = ROPE_THETA ** (-jnp.arange(0, rot, 2, dtype=F32) / rot)
    ang = positions.astype(F32)[:, None] * inv[None, :]
    cos, sin = jnp.cos(ang), jnp.sin(ang)
    n = positions.shape[0]
    pad = jnp.zeros((n, DIFF_HEAD_DIM - rot), F32)
    zeros8 = jnp.zeros((n, DIFF_ROT_HALF), F32)
    rc = jnp.concatenate([cos, cos, pad + 1.0], axis=1)
    ra = jnp.concatenate([-sin, zeros8, pad], axis=1)
    rb = jnp.concatenate([zeros8, sin, pad], axis=1)
    return tuple(jnp.concatenate([t, t], axis=1) for t in (rc, ra, rb))


def _block_diag_ones(width):
    idx = jnp.arange(HEAD_W) // width
    return (idx[:, None] == idx[None, :]).astype(BF16)


def kernel(x, meta_tokens, norm_mix_g, w_in, b_forget, b_gate, diff_q_norm_g, diff_k_norm_g,
           fox_q_norm_g, fox_k_norm_g, lambda_q1, lambda_k1, lambda_q2, lambda_k2, diff_subln_g,
           w_branch_diff, w_branch_fox, w_out, norm_peer_g, peer_w_query, peer_sub_keys,
           peer_expert_u, peer_expert_v):
    b, seq, d = x.shape
    m = b * seq
    x2d = x.reshape(m, d)
    meta_pad = jnp.pad(meta_tokens.astype(F32), ((0, LANES - N_META), (0, 0)))

    qkv_w = 6 * GROUP_W
    w0 = w_in[0]
    w_qkv = w0[:, :qkv_w].astype(BF16)
    wf = jnp.pad(w0[:, qkv_w:qkv_w + N_HEADS], ((0, 0), (0, LANES - N_HEADS)))
    wf_hi, wf_lo = _split_bf16(wf)
    w_ga = w0[:, qkv_w + N_HEADS:qkv_w + N_HEADS + d].astype(BF16)
    w_gb = w0[:, qkv_w + N_HEADS + d:].astype(BF16)

    two = lambda g: jnp.concatenate([g, g]).reshape(1, HEAD_W)
    gq, gk = two(diff_q_norm_g[0]), two(diff_k_norm_g[0])
    gfq, gfk = fox_q_norm_g[0].reshape(1, HEAD_W), fox_k_norm_g[0].reshape(1, HEAD_W)
    bd64, bd128 = _block_diag_ones(DIFF_HEAD_DIM), _block_diag_ones(HEAD_W)
    g_mix = norm_mix_g[0].reshape(1, d)

    tm = min(512, seq)
    proj = functools.partial(_proj_call, g_row=g_mix, w_qkv=w_qkv, wf_hi=wf_hi, wf_lo=wf_lo,
                             gq=gq, gk=gk, gfq=gfq, gfk=gfk, bd64=bd64, bd128=bd128)
    rc, ra, rb = _rope_tables(N_META + jnp.arange(seq))
    h, p, f = proj(x2d, rc=rc, ra=ra, rb=rb, tm=tm)
    rc, ra, rb = _rope_tables(jnp.arange(LANES))
    _, p_meta, f_meta = proj(meta_pad, rc=rc, ra=ra, rb=rb, tm=LANES)

    lk = seq + LANES
    f_real = f[:, :N_HEADS].reshape(b, seq, N_HEADS).transpose(0, 2, 1)
    f_m = jnp.broadcast_to(f_meta[:N_META, :N_HEADS].T[None], (b, N_HEADS, N_META))
    f_t = jnp.concatenate([f_m, f_real, jnp.zeros((b, N_HEADS, LANES - N_META), F32)], axis=2)
    tri = (jnp.arange(LANES)[:, None] <= jnp.arange(LANES)[None, :]).astype(BF16)
    c = _cumsum_call(f_t, b_forget[0].reshape(N_HEADS, 1), tri)
    c_real = c[:, :, N_META:N_META + seq]
    cq = c_real.reshape(b, N_HEADS, 1, seq)
    ckb = jnp.broadcast_to(c_real[..., None], (b, N_HEADS, seq, LANES))
    cm = jnp.pad(c[:, :, :N_META], ((0, 0), (0, 0), (0, LANES - N_META)))
    cmb = jnp.broadcast_to(cm[..., None], (b, N_HEADS, LANES, LANES))

    def values_t(group):
        cols = slice(group * GROUP_W, (group + 1) * GROUP_W)
        vt = p[:, cols].reshape(b, seq, N_HEADS, HEAD_W).transpose(0, 2, 3, 1)
        vtm = p_meta[:, cols].reshape(LANES, N_HEADS, HEAD_W).transpose(1, 2, 0)
        return vt, vtm

    tq = min(512, seq)
    head4 = lambda bi, hh, i: (bi, hh, 0, 0)
    vt, vtm = values_t(5)
    o_fox = _attn_call(
        _fox_kernel, p, p_meta, vt, vtm, 3, 4, (cq, ckb, cmb),
        [pl.BlockSpec((1, 1, 1, tq), lambda bi, hh, i: (bi, hh, 0, i)),
         pl.BlockSpec((1, 1, seq, LANES), head4),
         pl.BlockSpec((1, 1, LANES, LANES), head4)],
        1, seq, tq, "fox_attn")
    vec = lambda a: a[0].reshape(1, -1)
    const3 = lambda bi, hh, i: (0, 0)
    vt, vtm = values_t(2)
    o_diff = _attn_call(
        _diff_kernel, p, p_meta, vt, vtm, 0, 1,
        (vec(lambda_q1), vec(lambda_k1), vec(lambda_q2), vec(lambda_k2), vec(diff_subln_g)),
        [pl.BlockSpec((1, DIFF_HEAD_DIM), const3)] * 4 + [pl.BlockSpec((1, HEAD_W), const3)],
        2, seq, tq, "diff_attn")

    merged = _merge_call(o_diff, o_fox, h, w_branch_diff[0].astype(BF16), w_branch_fox[0].astype(BF16),
                         w_ga, w_gb, b_gate[0, 0].reshape(1, d), b_gate[0, 1].reshape(1, d),
                         tm=tm, tn=512)
    resid, h2 = _out_call(merged, w_out[0].astype(BF16), x2d, norm_peer_g[0].reshape(1, d),
                          tm=min(256, seq))

    sk = peer_sub_keys[0]
    zk = jnp.zeros_like(sk[:, 0])
    kbd = jnp.concatenate([jnp.concatenate([sk[:, 0], zk], axis=2),
                           jnp.concatenate([zk, sk[:, 1]], axis=2)], axis=1)
    kbd_hi, kbd_lo = _split_bf16(kbd)
    s1, e1, s2, e2, thr = _select_call(h2, peer_w_query[0].astype(BF16), kbd_hi, kbd_lo,
                                       tm=min(256, seq))

    out = _peer_call(h2, peer_expert_u[0].astype(BF16), peer_expert_v[0].astype(BF16).T,
                     s1, e1, s2, e2, thr, resid, tm=tm, ce=1024)
    return out.reshape(b, seq, d)
```
